```python
import jax
import jax.numpy as jnp
from jax import lax
import numpy as np

D_MODEL = 1024
BATCH = 8
SEQ = 4096
DEPTH = 2
DEC_BATCH = 16
DEC_SEQ = 16
PAST_LEN = 4096

CHUNK = 64
N_META = 16
N_A_LAYERS = DEPTH // 2
N_B_LAYERS = DEPTH - N_A_LAYERS
N_DENSE = (DEPTH + 1) // 2
N_MOE = DEPTH // 2
RWKV_HEAD = 64
RWKV_HEADS = D_MODEL // RWKV_HEAD
LORA_DECAY = 64
LORA_A = 64
LORA_GATE = 128
GN_EPS = 64e-5
HEAD_DIM = 64
N_HEADS = D_MODEL // HEAD_DIM
N_KV_HEADS = N_HEADS // 8
GROUP = N_HEADS // N_KV_HEADS
WINDOW = 128
WIN_CHUNKS = WINDOW // CHUNK
ROPE_THETA = 10000.0
FFN_DENSE = 2816
N_EXPERTS = 8
TOP_K = 2
FFN_EXPERT = 3584
RMS_EPS = 1e-5

kernel_name = 'yoco_rwkv7_swa_sink_moe_stream_step'


def _rmsnorm(x, g):
    xf = x.astype(jnp.float32)
    y = xf * lax.rsqrt(jnp.mean(xf * xf, axis=-1, keepdims=True) + RMS_EPS)
    return (y * g.astype(jnp.float32)).astype(x.dtype)


def _rope(x, pos):
    half = HEAD_DIM // 2
    inv = 1.0 / (ROPE_THETA ** (jnp.arange(half, dtype=jnp.float32) / half))
    ang = pos.astype(jnp.float32)[:, None] * inv[None, :]
    cos = jnp.cos(ang)[None, :, None, :]
    sin = jnp.sin(ang)[None, :, None, :]
    xf = x.astype(jnp.float32)
    x1, x2 = xf[..., :half], xf[..., half:]
    return jnp.concatenate([x1 * cos - x2 * sin, x2 * cos + x1 * sin], axis=-1).astype(x.dtype)


def _wkv7_scan(r, w, k, v, a, b, s0):
    def step(s, inp):
        r_t, w_t, k_t, v_t, a_t, b_t = inp
        sa = jnp.einsum('bhij,bhj->bhi', s, a_t)
        s = s * w_t[:, :, None, :] + sa[..., None] * b_t[:, :, None, :] + v_t[..., None] * k_t[:, :, None, :]
        y = jnp.einsum('bhij,bhj->bhi', s, r_t)
        return s, y
    xs = tuple(jnp.swapaxes(t, 0, 1) for t in (r, w, k, v, a, b))
    s, ys = lax.scan(step, s0, xs)
    return jnp.swapaxes(ys, 0, 1), s


def _rwkv7_time_mix(xn, x_prev0, wkv0, mu, w0, w1, w2, a0, a1, a2, g1, g2,
                    k_k, k_a, r_k, w_r, w_k, w_v, w_o, lnx_w, lnx_b):
    f32 = jnp.float32
    bsz, t_len, _ = xn.shape
    x_prev = jnp.concatenate([x_prev0[:, None, :].astype(xn.dtype), xn[:, :-1]], axis=1)
    dx = x_prev - xn
    xr, xw, xk, xv, xa, xg = [xn + dx * mu[i] for i in range(6)]
    r = xr @ w_r
    k = xk @ w_k
    v = xv @ w_v
    w_log = -jax.nn.softplus(-(w0.astype(f32) + (jnp.tanh(xw @ w1) @ w2).astype(f32))) - 0.5
    decay = jnp.exp(-jnp.exp(w_log))
    a = jax.nn.sigmoid(a0.astype(f32) + ((xa @ a1) @ a2).astype(f32))
    g = jax.nn.sigmoid(xg @ g1) @ g2
    hs = lambda t: t.reshape(bsz, t_len, RWKV_HEADS, RWKV_HEAD)
    kf = k.astype(f32)
    kk = hs(kf * k_k.astype(f32))
    kk = kk / jnp.maximum(jnp.sqrt(jnp.sum(kk * kk, axis=-1, keepdims=True)), 1e-12)
    k_mod = hs(kf * (1.0 + (a - 1.0) * k_a.astype(f32)))
    rf = hs(r.astype(f32))
    vf = hs(v.astype(f32))
    y, wkv = _wkv7_scan(rf, hs(decay), k_mod, vf, -kk, kk * hs(a), wkv0)
    mean = jnp.mean(y, axis=-1, keepdims=True)
    var = jnp.mean((y - mean) ** 2, axis=-1, keepdims=True)
    y = ((y - mean) * lax.rsqrt(var + GN_EPS)).reshape(bsz, t_len, D_MODEL)
    y = y * lnx_w.astype(f32) + lnx_b.astype(f32)
    bonus = jnp.sum(rf * k_mod * r_k.astype(f32), axis=-1, keepdims=True) * vf
    y = y + bonus.reshape(bsz, t_len, D_MODEL)
    out = (y.astype(xn.dtype) * g) @ w_o
    return out, wkv, xn[:, -1]


def _sink_attention(q, k, v, mask, sinks):
    s = jnp.einsum('bnqhgd,bnkhd->bnhgqk', q, k).astype(jnp.float32) * (HEAD_DIM ** -0.5)
    s = jnp.where(mask[None, :, None, None, None, :], s, -jnp.inf)
    sink = sinks.astype(jnp.float32).reshape(1, 1, N_KV_HEADS, GROUP, 1, 1)
    m = jnp.maximum(jnp.max(s, axis=-1, keepdims=True), sink)
    p = jnp.exp(s - m)
    denom = jnp.sum(p, axis=-1, keepdims=True) + jnp.exp(sink - m)
    return jnp.einsum('bnhgqk,bnkhd->bnqhgd', (p / denom).astype(v.dtype), v)


def _shared_kv(h, norm_kv, w_k, w_v, pos):
    bsz, t_len, _ = h.shape
    hn = _rmsnorm(h, norm_kv)
    k = _rope((hn @ w_k).reshape(bsz, t_len, N_KV_HEADS, HEAD_DIM), pos)
    v = (hn @ w_v).reshape(bsz, t_len, N_KV_HEADS, HEAD_DIM)
    return k, v


def _q_heads(hn, w_q, pos):
    bsz, t_len, _ = hn.shape
    q = _rope((hn @ w_q).reshape(bsz, t_len, N_HEADS, HEAD_DIM), pos)
    return q.reshape(bsz, t_len, N_KV_HEADS, GROUP, HEAD_DIM)


def _swa_prompt(hn, k, v, pos, w_q, sinks, w_o):
    bsz, l_len, _ = hn.shape
    t_len = l_len - N_META
    n_c = t_len // CHUNK
    q = _q_heads(hn, w_q, pos)
    q_m, q_r = q[:, :N_META], q[:, N_META:]
    k_m, k_r = k[:, :N_META], k[:, N_META:]
    v_m, v_r = v[:, :N_META], v[:, N_META:]
    o_m = _sink_attention(q_m[:, None], k_m[:, None], v_m[:, None],
                          jnp.ones((1, N_META), bool), sinks)[:, 0]

    def band(t):
        tp = jnp.pad(t, ((0, 0), (WIN_CHUNKS * CHUNK, 0), (0, 0), (0, 0)))
        tp = tp.reshape(bsz, n_c + WIN_CHUNKS, CHUNK, N_KV_HEADS, HEAD_DIM)
        return jnp.concatenate([tp[:, j:j + n_c] for j in range(WIN_CHUNKS + 1)], axis=2)

    def with_meta(tm, tb):
        return jnp.concatenate([jnp.broadcast_to(tm[:, None], (bsz, n_c) + tm.shape[1:]), tb], axis=2)

    k_ctx = with_meta(k_m, band(k_r))
    v_ctx = with_meta(v_m, band(v_r))
    band_chunk = jnp.arange(n_c)[:, None] - WIN_CHUNKS + jnp.arange(WIN_CHUNKS + 1)[None, :]
    mask = jnp.concatenate([jnp.ones((n_c, N_META), bool),
                            jnp.repeat(band_chunk >= 0, CHUNK, axis=1)], axis=1)
    o_r = _sink_attention(q_r.reshape(bsz, n_c, CHUNK, N_KV_HEADS, GROUP, HEAD_DIM),
                          k_ctx, v_ctx, mask, sinks)
    o_r = o_r.reshape(bsz, t_len, N_KV_HEADS, GROUP, HEAD_DIM)
    o = jnp.concatenate([o_m, o_r], axis=1).reshape(bsz, l_len, N_HEADS * HEAD_DIM)
    return o @ w_o


def _swa_sample(hn, k_new, v_new, meta_k, meta_v, win_k, win_v, pos, w_q, sinks, w_o):
    bsz, t_len, _ = hn.shape
    q = _q_heads(hn, w_q, pos)
    dt = k_new.dtype
    k_ctx = jnp.concatenate([meta_k.astype(dt), win_k.astype(dt), k_new], axis=1)
    v_ctx = jnp.concatenate([meta_v.astype(dt), win_v.astype(dt), v_new], axis=1)
    mask = jnp.ones((1, k_ctx.shape[1]), bool)
    o = _sink_attention(q[:, None], k_ctx[:, None], v_ctx[:, None], mask, sinks)[:, 0]
    return o.reshape(bsz, t_len, N_HEADS * HEAD_DIM) @ w_o


def _swiglu(x, w_gate, w_up, w_down):
    return (jax.nn.silu(x @ w_gate) * (x @ w_up)) @ w_down


def _moe_swiglu(x, router, w_gate, w_up, w_down):
    logits = (x @ router).astype(jnp.float32)
    top_v, top_i = lax.top_k(logits, TOP_K)
    gates = jax.nn.softmax(top_v, axis=-1)
    combine = jnp.sum(jax.nn.one_hot(top_i, N_EXPERTS, dtype=jnp.float32) * gates[..., None], axis=-2)
    out = jnp.zeros(x.shape, jnp.float32)
    for e in range(N_EXPERTS):
        out = out + combine[..., e:e + 1] * _swiglu(x, w_gate[e], w_up[e], w_down[e]).astype(jnp.float32)
    return out.astype(x.dtype)


def setup_inputs(seed: int = 0) -> dict:
    key = jax.random.key(seed)
    ks = iter(jax.random.split(key, 64))
    f32 = jnp.float32

    def nrm(shape, scale):
        return scale * jax.random.normal(next(ks), shape, f32)

    D = D_MODEL
    n_win = min(WINDOW, PAST_LEN)
    hd_all = N_HEADS * HEAD_DIM
    kv_all = N_KV_HEADS * HEAD_DIM
    return {
        'x_prompt': nrm((BATCH, SEQ, D), 1.0),
        'x_sample': nrm((DEC_BATCH, DEC_SEQ, D), 1.0),
        'state_wkv': nrm((DEC_BATCH, N_A_LAYERS, RWKV_HEADS, RWKV_HEAD, RWKV_HEAD), 0.5),
        'state_shift': nrm((DEC_BATCH, N_A_LAYERS, D), 1.0),
        'cache_meta_k': nrm((DEC_BATCH, N_META, N_KV_HEADS, HEAD_DIM), 1.0),
        'cache_meta_v': nrm((DEC_BATCH, N_META, N_KV_HEADS, HEAD_DIM), 1.0),
        'cache_win_k': nrm((DEC_BATCH, n_win, N_KV_HEADS, HEAD_DIM), 1.0),
        'cache_win_v': nrm((DEC_BATCH, n_win, N_KV_HEADS, HEAD_DIM), 1.0),
        'meta_tokens': nrm((N_META, D), 1.0),
        'norm_mix': 1.0 + nrm((DEPTH, D), 0.05),
        'norm_ffn': 1.0 + nrm((DEPTH, D), 0.05),
        'norm_kv': 1.0 + nrm((D,), 0.05),
        'norm_final': 1.0 + nrm((D,), 0.05),
        'rwkv_mu': jax.random.uniform(next(ks), (N_A_LAYERS, 6, D), f32),
        'rwkv_w0': -3.5 + nrm((N_A_LAYERS, D), 1.0),
        'rwkv_w1': nrm((N_A_LAYERS, D, LORA_DECAY), D ** -0.5),
        'rwkv_w2': nrm((N_A_LAYERS, LORA_DECAY, D), 0.1 * LORA_DECAY ** -0.5),
        'rwkv_a0': nrm((N_A_LAYERS, D), 0.1),
        'rwkv_a1': nrm((N_A_LAYERS, D, LORA_A), D ** -0.5),
        'rwkv_a2': nrm((N_A_LAYERS, LORA_A, D), 0.1 * LORA_A ** -0.5),
        'rwkv_g1': nrm((N_A_LAYERS, D, LORA_GATE), D ** -0.5),
        'rwkv_g2': nrm((N_A_LAYERS, LORA_GATE, D), LORA_GATE ** -0.5),
        'rwkv_kk': 0.85 + nrm((N_A_LAYERS, D), 0.05),
        'rwkv_ka': 1.0 + nrm((N_A_LAYERS, D), 0.05),
        'rwkv_rk': nrm((N_A_LAYERS, RWKV_HEADS, RWKV_HEAD), 0.1),
        'rwkv_wr': nrm((N_A_LAYERS, D, D), D ** -0.5),
        'rwkv_wk': nrm((N_A_LAYERS, D, D), D ** -0.5),
        'rwkv_wv': nrm((N_A_LAYERS, D, D), D ** -0.5),
        'rwkv_wo': nrm((N_A_LAYERS, D, D), D ** -0.5),
        'rwkv_lnx_w': 1.0 + nrm((N_A_LAYERS, D), 0.05),
        'rwkv_lnx_b': nrm((N_A_LAYERS, D), 0.02),
        'attn_wk': nrm((D, kv_all), D ** -0.5),
        'attn_wv': nrm((D, kv_all), D ** -0.5),
        'attn_wq': nrm((N_B_LAYERS, D, hd_all), D ** -0.5),
        'attn_sinks': nrm((N_B_LAYERS, N_HEADS), 0.5),
        'attn_wo': nrm((N_B_LAYERS, hd_all, D), hd_all ** -0.5),
        'ffn_w_gate': nrm((N_DENSE, D, FFN_DENSE), D ** -0.5),
        'ffn_w_up': nrm((N_DENSE, D, FFN_DENSE), D ** -0.5),
        'ffn_w_down': nrm((N_DENSE, FFN_DENSE, D), FFN_DENSE ** -0.5),
        'moe_router': nrm((N_MOE, D, N_EXPERTS), D ** -0.5),
        'moe_w_gate': nrm((N_MOE, N_EXPERTS, D, FFN_EXPERT), D ** -0.5),
        'moe_w_up': nrm((N_MOE, N_EXPERTS, D, FFN_EXPERT), D ** -0.5),
        'moe_w_down': nrm((N_MOE, N_EXPERTS, FFN_EXPERT, D), FFN_EXPERT ** -0.5),
    }


def reference(x_prompt, x_sample, state_wkv, state_shift, cache_meta_k, cache_meta_v,
              cache_win_k, cache_win_v, meta_tokens, norm_mix, norm_ffn, norm_kv, norm_final,
              rwkv_mu, rwkv_w0, rwkv_w1, rwkv_w2, rwkv_a0, rwkv_a1, rwkv_a2, rwkv_g1, rwkv_g2,
              rwkv_kk, rwkv_ka, rwkv_rk, rwkv_wr, rwkv_wk, rwkv_wv, rwkv_wo, rwkv_lnx_w, rwkv_lnx_b,
              attn_wk, attn_wv, attn_wq, attn_sinks, attn_wo,
              ffn_w_gate, ffn_w_up, ffn_w_down,
              moe_router, moe_w_gate, moe_w_up, moe_w_down):
    b_p, t_p, _ = x_prompt.shape
    t_s = x_sample.shape[1]
    h_p = jnp.concatenate([jnp.broadcast_to(meta_tokens.astype(x_prompt.dtype)[None], (b_p, N_META, D_MODEL)),
                           x_prompt], axis=1)
    h_s = x_sample
    pos_p = jnp.arange(N_META + t_p, dtype=jnp.int32)
    pos_s = N_META + PAST_LEN + jnp.arange(t_s, dtype=jnp.int32)
    wkv_p, shift_p, wkv_s, shift_s = [], [], [], []
    for layer in range(DEPTH):
        if layer < N_A_LAYERS:
            i = layer
            rw = (rwkv_mu[i], rwkv_w0[i], rwkv_w1[i], rwkv_w2[i], rwkv_a0[i], rwkv_a1[i], rwkv_a2[i],
                  rwkv_g1[i], rwkv_g2[i], rwkv_kk[i], rwkv_ka[i], rwkv_rk[i], rwkv_wr[i], rwkv_wk[i],
                  rwkv_wv[i], rwkv_wo[i], rwkv_lnx_w[i], rwkv_lnx_b[i])
            o, s, last = _rwkv7_time_mix(
                _rmsnorm(h_p, norm_mix[layer]), jnp.zeros((b_p, D_MODEL), h_p.dtype),
                jnp.zeros((b_p, RWKV_HEADS, RWKV_HEAD, RWKV_HEAD), jnp.float32), *rw)
            h_p = h_p + o
            wkv_p.append(s)
            shift_p.append(last)
            o, s, last = _rwkv7_time_mix(
                _rmsnorm(h_s, norm_mix[layer]), state_shift[:, i],
                state_wkv[:, i].astype(jnp.float32), *rw)
            h_s = h_s + o
            wkv_s.append(s)
            shift_s.append(last)
        else:
            j = layer - N_A_LAYERS
            if j == 0:
                k_p, v_p = _shared_kv(h_p, norm_kv, attn_wk, attn_wv, pos_p)
                k_s, v_s = _shared_kv(h_s, norm_kv, attn_wk, attn_wv, pos_s)
            h_p = h_p + _swa_prompt(_rmsnorm(h_p, norm_mix[layer]), k_p, v_p, pos_p,
                                    attn_wq[j], attn_sinks[j], attn_wo[j])
            h_s = h_s + _swa_sample(_rmsnorm(h_s, norm_mix[layer]), k_s, v_s, cache_meta_k, cache_meta_v,
                                    cache_win_k, cache_win_v, pos_s, attn_wq[j], attn_sinks[j], attn_wo[j])
        if layer % 2 == 0:
            f = layer // 2
            h_p = h_p + _swiglu(_rmsnorm(h_p, norm_ffn[layer]), ffn_w_gate[f], ffn_w_up[f], ffn_w_down[f])
            h_s = h_s + _swiglu(_rmsnorm(h_s, norm_ffn[layer]), ffn_w_gate[f], ffn_w_up[f], ffn_w_down[f])
        else:
            e = layer // 2
            h_p = h_p + _moe_swiglu(_rmsnorm(h_p, norm_ffn[layer]), moe_router[e], moe_w_gate[e],
                                    moe_w_up[e], moe_w_down[e])
            h_s = h_s + _moe_swiglu(_rmsnorm(h_s, norm_ffn[layer]), moe_router[e], moe_w_gate[e],
                                    moe_w_up[e], moe_w_down[e])
    y_prompt = _rmsnorm(h_p, norm_final)[:, N_META:]
    y_sample = _rmsnorm(h_s, norm_final)
    n_win_p = min(WINDOW, t_p)
    new_wkv_p = jnp.stack(wkv_p, axis=1).astype(x_prompt.dtype)
    new_shift_p = jnp.stack(shift_p, axis=1)
    new_meta_k_p = k_p[:, :N_META]
    new_meta_v_p = v_p[:, :N_META]
    new_win_k_p = k_p[:, k_p.shape[1] - n_win_p:]
    new_win_v_p = v_p[:, v_p.shape[1] - n_win_p:]
    new_wkv_s = jnp.stack(wkv_s, axis=1).astype(state_wkv.dtype)
    new_shift_s = jnp.stack(shift_s, axis=1)
    return (y_prompt, y_sample, new_wkv_p, new_shift_p, new_meta_k_p, new_meta_v_p, new_win_k_p, new_win_v_p,
            new_wkv_s, new_shift_s, k_s, v_s)
```

```python
import functools

import jax
import jax.numpy as jnp
import numpy as np
from jax import lax
from jax.experimental import pallas as pl
from jax.experimental.pallas import tpu as pltpu

F32 = jnp.float32
BF16 = jnp.bfloat16

D = 1024
HD = 64
NH = D // HD
NKV = 2
GROUP = NH // NKV
CHUNK = 64
N_META = 16
WINDOW = 128
PAST_LEN = 4096
RMS_EPS = 1e-5
GN_EPS = 64e-5
ROPE_THETA = 10000.0
N_EXPERTS = 8
LANES = 128
NEG_INF = float("-inf")
VMEM_LIMIT = 56 * 1024 * 1024


def _bf(x):
    return x.astype(BF16)


def _dot(a, b):
    return jnp.dot(a, b, preferred_element_type=F32)


def _dot_nt(a, b):
    return lax.dot_general(a, b, (((1,), (1,)), ((), ())), preferred_element_type=F32)


def _dot_tn(a, b):
    return lax.dot_general(a, b, (((0,), (0,)), ((), ())), preferred_element_type=F32)


def _rms(x, g):
    return x * lax.rsqrt(jnp.mean(x * x, axis=-1, keepdims=True) + RMS_EPS) * g


def _sigmoid(x):
    return 1.0 / (1.0 + jnp.exp(-x))


def _const_spec(shape):
    n = len(shape)
    return pl.BlockSpec(shape, lambda *_: (0,) * n)


def _params(sem):
    return pltpu.CompilerParams(dimension_semantics=sem, vmem_limit_bytes=VMEM_LIMIT)


def _norm_rows_kernel(x_ref, g_ref, o_ref):
    o_ref[...] = _rms(x_ref[...], g_ref[...])


def _norm_rows(x, g):
    return pl.pallas_call(
        _norm_rows_kernel, out_shape=jax.ShapeDtypeStruct(x.shape, F32), name="norm_rows")(x, g)


def _rwkv_pre_kernel(short, tm, stream_len, h_ref, sv_ref, vec_ref, wr, wk, wv, w1, w2, a1, a2, g1, g2,
                     r_o, k_o, v_o, a_o, g_o, lw_o, carry):
    vec = vec_ref[...]
    xn = _rms(h_ref[...], vec[8:9])
    prev = pltpu.roll(xn, 1, 0)
    row = lax.broadcasted_iota(jnp.int32, (tm, 1), 0)
    if short:
        prev = jnp.where(row % stream_len == 0, sv_ref[...], prev)
    else:
        i = pl.program_id(0)

        @pl.when(i == 0)
        def _():
            carry[...] = jnp.zeros_like(carry)

        first = jnp.where((i * tm) % stream_len == 0, sv_ref[0], carry[0:1, :])
        prev = jnp.where(row == 0, first, prev)
        carry[0:1, :] = xn[tm - 1:tm, :]
    dx = prev - xn

    def mix(j):
        return _bf(xn + dx * vec[j:j + 1])

    r_o[...] = _bf(_dot(mix(0), wr[...]))
    k_o[...] = _bf(_dot(mix(2), wk[...]))
    v_o[...] = _bf(_dot(mix(3), wv[...]))
    z = vec[6:7] + _dot(_bf(jnp.tanh(_dot(mix(1), w1[...]))), w2[...])
    softplus_neg = jnp.maximum(-z, 0.0) + jnp.log(1.0 + jnp.exp(-jnp.abs(z)))
    lw_o[...] = -jnp.exp(-softplus_neg - 0.5)
    a_o[...] = _bf(_sigmoid(vec[7:8] + _dot(_bf(_dot(mix(4), a1[...])), a2[...])))
    g_o[...] = _bf(_dot(_bf(_sigmoid(_dot(mix(5), g1[...]))), g2[...]))


def _rwkv_pre(h, sv, vec, ws, *, short, stream_len, tm):
    m = h.shape[0]
    grid = (m // tm,)
    row_spec = pl.BlockSpec((tm, D), lambda i: (i, 0))
    if short:
        sv_spec = pl.BlockSpec((tm, D), lambda i: (i, 0))
    else:
        sv_spec = pl.BlockSpec((1, 1, D), lambda i: ((i * tm) // stream_len, 0, 0))
    in_specs = [row_spec, sv_spec, _const_spec(vec.shape)] + [_const_spec(w.shape) for w in ws]
    out_shape = [jax.ShapeDtypeStruct((m, D), BF16)] * 5 + [jax.ShapeDtypeStruct((m, D), F32)]
    return pl.pallas_call(
        functools.partial(_rwkv_pre_kernel, short, tm, stream_len),
        grid=grid, in_specs=in_specs, out_specs=[row_spec] * 6, out_shape=out_shape,
        scratch_shapes=[pltpu.VMEM((8, D), F32)],
        compiler_params=_params(("arbitrary",)), name="rwkv_pre")(h, sv, vec, *ws)


def _wkv_kernel(c_len, r_ref, k_ref, v_ref, a_ref, g_ref, lw_ref, s0_ref, vec_ref, zg_ref, sout_ref, state):
    c = pl.program_id(1)

    @pl.when(c == 0)
    def _():
        state[...] = s0_ref[0]

    row = lax.broadcasted_iota(jnp.int32, (c_len, c_len), 0)
    col = lax.broadcasted_iota(jnp.int32, (c_len, c_len), 1)
    incl = row >= col
    strict = row > col
    tri_ones = jnp.where(incl, 1.0, 0.0).astype(BF16)
    eye = jnp.where(row == col, 1.0, 0.0).astype(F32)
    vec = vec_ref[...]
    n_double = int(np.log2(c_len)) - 1

    for h in range(NH):
        sl = slice(HD * h, HD * (h + 1))
        lw = lw_ref[:, sl]
        lw_hi = _bf(lw)
        lw_lo = _bf(lw - lw_hi.astype(F32))
        cw = _dot(tri_ones, lw_hi) + _dot(tri_ones, lw_lo)
        cw_last = cw[c_len - 1:c_len, :]
        w_in = jnp.exp(cw)
        w_ex = jnp.exp(cw - lw)
        w_inv = jnp.exp(-cw)
        w_rem = jnp.exp(cw_last - cw)

        r = r_ref[:, sl].astype(F32)
        k_raw = k_ref[:, sl].astype(F32)
        v = v_ref[:, sl]
        a_sig = a_ref[:, sl].astype(F32)
        kk = k_raw * vec[0:1, sl]
        kk = kk / jnp.maximum(jnp.sqrt(jnp.sum(kk * kk, axis=-1, keepdims=True)), 1e-12)
        k_mod = k_raw * (1.0 + (a_sig - 1.0) * vec[1:2, sl])
        b = kk * a_sig

        a_t = _bf(-kk * w_ex)
        r_t = _bf(r * w_in)
        lhs = jnp.concatenate([a_t, r_t], axis=0)
        rhs = jnp.concatenate([_bf(b * w_inv), _bf(k_mod * w_inv)], axis=0)
        gram = _dot_nt(lhs, rhs)
        n_ab = jnp.where(strict, gram[:c_len, :c_len], 0.0)
        l_ak = jnp.where(strict, gram[:c_len, c_len:], 0.0)
        m_rb = jnp.where(incl, gram[c_len:, :c_len], 0.0)
        m_rk = jnp.where(incl, gram[c_len:, c_len:], 0.0)

        s_prev = state[h]
        s_bf = _bf(s_prev)
        rhs_u = _dot_nt(a_t, s_bf) + _dot(_bf(l_ak), v)
        t_inv = eye + n_ab
        pw = n_ab
        for _ in range(n_double):
            pw_bf = _bf(pw)
            pw = _dot(pw_bf, pw_bf)
            t_inv = t_inv + _dot(_bf(t_inv), _bf(pw))
        u = _dot(_bf(t_inv), _bf(rhs_u))
        u_bf = _bf(u)
        y = _dot_nt(r_t, s_bf) + _dot(_bf(m_rb), u_bf) + _dot(_bf(m_rk), v)

        uv = jnp.concatenate([u_bf, v], axis=0)
        bk = jnp.concatenate([_bf(b * w_rem), _bf(k_mod * w_rem)], axis=0)
        state[h] = s_prev * jnp.exp(cw_last) + _dot_tn(uv, bk)

        mean = jnp.mean(y, axis=-1, keepdims=True)
        yc = y - mean
        var = jnp.mean(yc * yc, axis=-1, keepdims=True)
        z = yc * lax.rsqrt(var + GN_EPS) * vec[3:4, sl] + vec[4:5, sl]
        bonus = jnp.sum(r * k_mod * vec[2:3, sl], axis=-1, keepdims=True) * v.astype(F32)
        zg_ref[:, sl] = _bf((z + bonus) * g_ref[:, sl].astype(F32))

    @pl.when(c == pl.num_programs(1) - 1)
    def _():
        sout_ref[0] = state[...]


def _wkv(r, k, v, a, g, lw, s0, vec, *, n_streams, n_chunks, c_len, shared_s0):
    m = r.shape[0]
    row_spec = pl.BlockSpec((c_len, D), lambda s, c: (s * n_chunks + c, 0))
    if shared_s0:
        s0_spec = pl.BlockSpec((1, NH, HD, HD), lambda s, c: (0, 0, 0, 0))
    else:
        s0_spec = pl.BlockSpec((1, NH, HD, HD), lambda s, c: (s, 0, 0, 0))
    sout_spec = pl.BlockSpec((1, NH, HD, HD), lambda s, c: (s, 0, 0, 0))
    return pl.pallas_call(
        functools.partial(_wkv_kernel, c_len),
        grid=(n_streams, n_chunks),
        in_specs=[row_spec] * 6 + [s0_spec, _const_spec(vec.shape)],
        out_specs=[row_spec, sout_spec],
        out_shape=[jax.ShapeDtypeStruct((m, D), BF16),
                   jax.ShapeDtypeStruct((n_streams, NH, HD, HD), F32)],
        scratch_shapes=[pltpu.VMEM((NH, HD, HD), F32)],
        compiler_params=_params(("arbitrary", "arbitrary")), name="wkv")(r, k, v, a, g, lw, s0, vec)


def _post_ffn_kernel(n_fc, h_ref, zg_ref, wo_ref, gn_ref, wg_ref, wu_ref, wd_ref, o_ref, acc, xn_s):
    h1 = h_ref[...] + _dot(zg_ref[...], wo_ref[...])
    acc[...] = h1
    xn_s[...] = _bf(_rms(h1, gn_ref[...]))

    def body(c, carry):
        xn = xn_s[...]
        gate = _dot(xn, wg_ref[c])
        up = _dot(xn, wu_ref[c])
        acc[...] += _dot(_bf(gate * _sigmoid(gate) * up), wd_ref[c])
        return carry

    lax.fori_loop(0, n_fc, body, 0)
    o_ref[...] = acc[...]


def _post_ffn(h, zg, wo, gn, wg, wu, wd, *, tm):
    m = h.shape[0]
    n_fc = wg.shape[0]
    row_spec = pl.BlockSpec((tm, D), lambda i: (i, 0))
    wspecs = [_const_spec(w.shape) for w in (wo, gn, wg, wu, wd)]
    return pl.pallas_call(
        functools.partial(_post_ffn_kernel, n_fc),
        grid=(m // tm,), in_specs=[row_spec, row_spec] + wspecs, out_specs=row_spec,
        out_shape=jax.ShapeDtypeStruct((m, D), F32),
        scratch_shapes=[pltpu.VMEM((tm, D), F32), pltpu.VMEM((tm, D), BF16)],
        compiler_params=_params(("arbitrary",)), name="post_ffn")(h, zg, wo, gn, wg, wu, wd)


def _kvq_kernel(tm, h_ref, cos_ref, sin_ref, gkv_ref, gq_ref, wkv_ref, wq_ref, k_o, v_o, q_o):
    x = h_ref[...]
    xn = x * lax.rsqrt(jnp.mean(x * x, axis=-1, keepdims=True) + RMS_EPS)
    kv = _dot(_bf(xn * gkv_ref[...]), wkv_ref[...])
    cos = cos_ref[...]
    sin = sin_ref[...]
    lane = lax.broadcasted_iota(jnp.int32, (tm, LANES), 1)
    first_half = (lane % HD) < (HD // 2)

    def rope(t):
        rot = jnp.where(first_half, pltpu.roll(t, LANES - HD // 2, 1), pltpu.roll(t, HD // 2, 1))
        return t * cos + rot * sin

    k_o[...] = rope(kv[:, :LANES])
    v_o[...] = kv[:, LANES:]
    q = _dot(_bf(xn * gq_ref[...]), wq_ref[...])
    for j in range(D // LANES):
        sl = slice(LANES * j, LANES * (j + 1))
        q_o[:, sl] = _bf(rope(q[:, sl]) * (HD ** -0.5))


def _kvq(h, cos, sin, gkv, gq, wkv, wq, *, tm, rope_blocks):
    m = h.shape[0]
    row_spec = pl.BlockSpec((tm, D), lambda i: (i, 0))
    tab_spec = pl.BlockSpec((tm, LANES), lambda i: (i % rope_blocks, 0))
    kv_spec = pl.BlockSpec((tm, LANES), lambda i: (i, 0))
    return pl.pallas_call(
        functools.partial(_kvq_kernel, tm),
        grid=(m // tm,),
        in_specs=[row_spec, tab_spec, tab_spec] + [_const_spec(w.shape) for w in (gkv, gq, wkv, wq)],
        out_specs=[kv_spec, kv_spec, row_spec],
        out_shape=[jax.ShapeDtypeStruct((m, LANES), F32), jax.ShapeDtypeStruct((m, LANES), F32),
                   jax.ShapeDtypeStruct((m, D), BF16)],
        compiler_params=_params(("arbitrary",)), name="kvq")(h, cos, sin, gkv, gq, wkv, wq)


def _attn_group(q_ref, row0, n_q, kv_head, ctx, sink_ref, o_ref):
    lane = lax.broadcasted_iota(jnp.int32, (n_q, LANES), 1)
    lo = lane < HD
    blocks, sinks = [], []
    for p in range(GROUP // 2):
        sl = slice(LANES * (kv_head * GROUP // 2 + p), LANES * (kv_head * GROUP // 2 + p + 1))
        qp = q_ref[row0:row0 + n_q, sl]
        zero = jnp.zeros_like(qp)
        blocks += [jnp.where(lo, qp, zero), jnp.where(lo, zero, qp)]
        for half in range(2):
            sinks.append(jnp.full((n_q, 1), sink_ref[kv_head * GROUP + 2 * p + half], F32))
    lhs = jnp.concatenate(blocks, axis=0)
    sink = jnp.concatenate(sinks, axis=0)
    scores = []
    for k_blk, _, valid in ctx:
        s = _dot_nt(lhs, k_blk)
        if valid is not None:
            s = jnp.where(valid, s, NEG_INF)
        scores.append(s)
    m = sink
    for s in scores:
        m = jnp.maximum(m, jnp.max(s, axis=-1, keepdims=True))
    den = jnp.exp(sink - m)
    acc = None
    for s, (_, v_blk, _) in zip(scores, ctx):
        p = jnp.exp(s - m)
        den = den + jnp.sum(p, axis=-1, keepdims=True)
        pv = _dot(_bf(p), v_blk)
        acc = pv if acc is None else acc + pv
    out = acc / den
    for p in range(GROUP // 2):
        sl = slice(LANES * (kv_head * GROUP // 2 + p), LANES * (kv_head * GROUP // 2 + p + 1))
        o_lo = out[(2 * p) * n_q:(2 * p + 1) * n_q]
        o_hi = out[(2 * p + 1) * n_q:(2 * p + 2) * n_q]
        o_ref[row0:row0 + n_q, sl] = _bf(jnp.where(lo, o_lo, o_hi))


def _attn_prompt_kernel(n_qc, q_ref, k_ref, v_ref, sink_ref, o_ref):
    i = pl.program_id(1)
    n_win = WINDOW + CHUNK
    col_chunk = lax.broadcasted_iota(jnp.int32, (1, n_win), 1) // CHUNK
    for cc in range(n_qc):
        c = i * n_qc + cc
        start = pl.multiple_of(N_META + CHUNK * c, 16)
        valid = (col_chunk + c) >= (WINDOW // CHUNK)
        for g in range(NKV):
            ctx = [(k_ref[0, g, 0:N_META, :], v_ref[0, g, 0:N_META, :], None),
                   (k_ref[0, g, pl.ds(start, n_win), :], v_ref[0, g, pl.ds(start, n_win), :], valid)]
            _attn_group(q_ref, CHUNK * cc, CHUNK, g, ctx, sink_ref, o_ref)


def _attn_prompt(q, k2, v2, sinks, *, n_streams, seq, n_qc):
    tq = n_qc * CHUNK
    nb = seq // tq
    lk = k2.shape[2]
    q_spec = pl.BlockSpec((tq, D), lambda b, i: (b * nb + i, 0))
    kv_spec = pl.BlockSpec((1, NKV, lk, LANES), lambda b, i: (b, 0, 0, 0))
    return pl.pallas_call(
        functools.partial(_attn_prompt_kernel, n_qc),
        grid=(n_streams, nb),
        in_specs=[q_spec, kv_spec, kv_spec, pl.BlockSpec(memory_space=pltpu.SMEM)],
        out_specs=q_spec, out_shape=jax.ShapeDtypeStruct(q.shape, BF16),
        compiler_params=_params(("arbitrary", "arbitrary")), name="attn_prompt")(q, k2, v2, sinks)


def _attn_short_kernel(n_q, q_ref, k_ref, v_ref, bias_ref, sink_ref, o_ref):
    valid = bias_ref[0] == 0.0
    for g in range(NKV):
        ctx = [(k_ref[0, g], v_ref[0, g], valid)]
        _attn_group(q_ref, 0, n_q, g, ctx, sink_ref, o_ref)


def _attn_short(q, k2, v2, bias, sinks, *, n_streams, n_q):
    lk = k2.shape[2]
    q_spec = pl.BlockSpec((n_q, D), lambda s: (s, 0))
    kv_spec = pl.BlockSpec((1, NKV, lk, LANES), lambda s: (s, 0, 0, 0))
    bias_spec = pl.BlockSpec((1, 1, lk), lambda s: (s, 0, 0))
    return pl.pallas_call(
        functools.partial(_attn_short_kernel, n_q),
        grid=(n_streams,),
        in_specs=[q_spec, kv_spec, kv_spec, bias_spec, pl.BlockSpec(memory_space=pltpu.SMEM)],
        out_specs=q_spec, out_shape=jax.ShapeDtypeStruct(q.shape, BF16),
        compiler_params=_params(("arbitrary",)), name="attn_short")(q, k2, v2, bias, sinks)


def _attn_out_router_kernel(tm, h_ref, o_ref, wo_ref, gn_ref, rh_ref, rl_ref, h_o, xn_o, comb_o):
    h3 = h_ref[...] + _dot(o_ref[...], wo_ref[...])
    h_o[...] = h3
    xn = _rms(h3, gn_ref[...])
    xn_o[...] = _bf(xn)
    x_hi = _bf(xn)
    x_lo = _bf(xn - x_hi.astype(F32))
    logits = _dot(x_hi, rh_ref[...]) + _dot(x_hi, rl_ref[...]) + _dot(x_lo, rh_ref[...])
    lane = lax.broadcasted_iota(jnp.int32, (tm, LANES), 1).astype(F32)
    logits = jnp.where(lane < N_EXPERTS, logits, NEG_INF)
    m1 = jnp.max(logits, axis=-1, keepdims=True)
    i1 = jnp.min(jnp.where(logits == m1, lane, float(LANES)), axis=-1, keepdims=True)
    rest = jnp.where(lane == i1, NEG_INF, logits)
    m2 = jnp.max(rest, axis=-1, keepdims=True)
    i2 = jnp.min(jnp.where(rest == m2, lane, float(LANES)), axis=-1, keepdims=True)
    e21 = jnp.exp(m2 - m1)
    g1 = 1.0 / (1.0 + e21)
    g2 = e21 / (1.0 + e21)
    comb_o[...] = jnp.where(lane == i1, g1, 0.0) + jnp.where(lane == i2, g2, 0.0)


def _attn_out_router(h, o, wo, gn, rh, rl, *, tm):
    m = h.shape[0]
    row_spec = pl.BlockSpec((tm, D), lambda i: (i, 0))
    lane_spec = pl.BlockSpec((tm, LANES), lambda i: (i, 0))
    return pl.pallas_call(
        functools.partial(_attn_out_router_kernel, tm),
        grid=(m // tm,),
        in_specs=[row_spec, row_spec] + [_const_spec(w.shape) for w in (wo, gn, rh, rl)],
        out_specs=[row_spec, row_spec, lane_spec],
        out_shape=[jax.ShapeDtypeStruct((m, D), F32), jax.ShapeDtypeStruct((m, D), BF16),
                   jax.ShapeDtypeStruct((m, LANES), F32)],
        compiler_params=_params(("arbitrary",)), name="attn_out_router")(h, o, wo, gn, rh, rl)


def _moe_kernel(tm, x_ref, comb_ref, h_ref, gfin_ref, wg_ref, wu_ref, wd_ref, o_ref, acc):
    e = pl.program_id(1)
    j = pl.program_id(2)

    @pl.when((e == 0) & (j == 0))
    def _():
        acc[...] = h_ref[...]

    x = x_ref[...]
    gate = _dot(x, wg_ref[0, 0])
    up = _dot(x, wu_ref[0, 0])
    y = _dot(_bf(gate * _sigmoid(gate) * up), wd_ref[0, 0])
    lane = lax.broadcasted_iota(jnp.int32, (tm, LANES), 1)
    weight = jnp.sum(jnp.where(lane == e, comb_ref[...], 0.0), axis=-1, keepdims=True)
    acc[...] += weight * y

    @pl.when((e == pl.num_programs(1) - 1) & (j == pl.num_programs(2) - 1))
    def _():
        o_ref[...] = _rms(acc[...], gfin_ref[...])


def _moe(x, comb, h, gfin, wg, wu, wd, *, tm):
    m = x.shape[0]
    n_e, n_fc, _, fc = wg.shape
    row_spec = pl.BlockSpec((tm, D), lambda i, e, j: (i, 0))
    lane_spec = pl.BlockSpec((tm, LANES), lambda i, e, j: (i, 0))
    win_spec = pl.BlockSpec((1, 1, D, fc), lambda i, e, j: (e, j, 0, 0))
    wout_spec = pl.BlockSpec((1, 1, fc, D), lambda i, e, j: (e, j, 0, 0))
    return pl.pallas_call(
        functools.partial(_moe_kernel, tm),
        grid=(m // tm, n_e, n_fc),
        in_specs=[row_spec, lane_spec, row_spec, _const_spec(gfin.shape), win_spec, win_spec, wout_spec],
        out_specs=row_spec, out_shape=jax.ShapeDtypeStruct((m, D), F32),
        scratch_shapes=[pltpu.VMEM((tm, D), F32)],
        compiler_params=_params(("arbitrary", "arbitrary", "arbitrary")), name="moe")(
            x, comb, h, gfin, wg, wu, wd)


def _rope_tables(pos):
    half = HD // 2
    inv = 1.0 / (ROPE_THETA ** (jnp.arange(half, dtype=F32) / half))
    ang = pos.astype(F32)[:, None] * inv[None, :]
    cos, sin = jnp.cos(ang), jnp.sin(ang)
    cos_t = jnp.concatenate([cos, cos, cos, cos], axis=-1)
    sin_t = jnp.concatenate([-sin, sin, -sin, sin], axis=-1)
    return cos_t, sin_t


def _dup_kv_heads(x):
    heads = [jnp.concatenate([x[..., HD * g:HD * (g + 1)]] * 2, axis=-1) for g in range(NKV)]
    return jnp.stack(heads, axis=-3)


def _chunk_cols(w, fc):
    *lead, d, f = w.shape
    w = w.reshape(*lead, d, f // fc, fc)
    return jnp.swapaxes(w, -2, -3)


def kernel(x_prompt, x_sample, state_wkv, state_shift, cache_meta_k, cache_meta_v, cache_win_k, cache_win_v, meta_tokens, norm_mix, norm_ffn, norm_kv, norm_final, rwkv_mu, rwkv_w0, rwkv_w1, rwkv_w2, rwkv_a0, rwkv_a1, rwkv_a2, rwkv_g1, rwkv_g2, rwkv_kk, rwkv_ka, rwkv_rk, rwkv_wr, rwkv_wk, rwkv_wv, rwkv_wo, rwkv_lnx_w, rwkv_lnx_b, attn_wk, attn_wv, attn_wq, attn_sinks, attn_wo, ffn_w_gate, ffn_w_up, ffn_w_down, moe_router, moe_w_gate, moe_w_up, moe_w_down):
    b_p, seq, _ = x_prompt.shape
    b_s, t_s, _ = x_sample.shape
    assert t_s == N_META and seq % (4 * CHUNK) == 0 and seq >= WINDOW
    n_short = 1 + b_s
    m_p = b_p * seq
    m_s = n_short * t_s
    tm = 512 if m_p % 512 == 0 else 256

    hp = x_prompt.reshape(m_p, D)
    hs = jnp.concatenate([meta_tokens, x_sample.reshape(b_s * t_s, D)], axis=0)

    g_mix0 = norm_mix[0][None, :]
    shift_rows = jnp.concatenate(
        [meta_tokens[N_META - 1:], x_prompt[:, -1], x_sample[:, -1],
         jnp.zeros((32 - 1 - b_p - b_s, D), F32)], axis=0)
    shift_norm = _norm_rows(shift_rows, g_mix0)
    xn_meta_last = shift_norm[0:1]
    new_shift_p = shift_norm[1:1 + b_p][:, None, :]
    new_shift_s = shift_norm[1 + b_p:1 + b_p + b_s][:, None, :]

    vec_pre = jnp.concatenate([rwkv_mu[0], rwkv_w0, rwkv_a0, g_mix0, jnp.zeros((7, D), F32)], axis=0)
    ws_pre = [_bf(w[0]) for w in (rwkv_wr, rwkv_wk, rwkv_wv, rwkv_w1, rwkv_w2, rwkv_a1, rwkv_a2,
                                  rwkv_g1, rwkv_g2)]
    vec_wkv = jnp.concatenate([rwkv_kk, rwkv_ka, rwkv_rk.reshape(1, D), rwkv_lnx_w, rwkv_lnx_b,
                               jnp.zeros((3, D), F32)], axis=0)

    sv_short = jnp.concatenate([jnp.zeros((1, D), F32), state_shift[:, 0]], axis=0)
    sv_short = jnp.concatenate([sv_short[:, None, :], jnp.zeros((n_short, t_s - 1, D), F32)],
                               axis=1).reshape(m_s, D)
    pre_s = _rwkv_pre(hs, sv_short, vec_pre, ws_pre, short=True, stream_len=t_s, tm=m_s)
    s0_short = jnp.concatenate([jnp.zeros((1, NH, HD, HD), F32), state_wkv[:, 0]], axis=0)
    zg_s, s_short = _wkv(*pre_s, s0_short, vec_wkv, n_streams=n_short, n_chunks=1, c_len=t_s,
                         shared_s0=False)

    sv_p = jnp.broadcast_to(xn_meta_last[None], (b_p, 1, D))
    pre_p = _rwkv_pre(hp, sv_p, vec_pre, ws_pre, short=False, stream_len=seq, tm=min(tm, 256))
    zg_p, s_prompt = _wkv(*pre_p, s_short[0:1], vec_wkv, n_streams=b_p, n_chunks=seq // CHUNK,
                          c_len=CHUNK, shared_s0=True)

    fc_d = 256
    wo0 = _bf(rwkv_wo[0])
    g_ffn0 = norm_ffn[0][None, :]
    wg_d = _bf(_chunk_cols(ffn_w_gate[0], fc_d))
    wu_d = _bf(_chunk_cols(ffn_w_up[0], fc_d))
    wd_d = _bf(ffn_w_down[0].reshape(-1, fc_d, D))
    hs = _post_ffn(hs, zg_s, wo0, g_ffn0, wg_d, wu_d, wd_d, tm=m_s)
    hp = _post_ffn(hp, zg_p, wo0, g_ffn0, wg_d, wu_d, wd_d, tm=tm)

    pos_s = jnp.concatenate(
        [jnp.arange(N_META, dtype=jnp.int32),
         jnp.tile(N_META + PAST_LEN + jnp.arange(t_s, dtype=jnp.int32), b_s)])
    cos_s, sin_s = _rope_tables(pos_s)
    cos_p, sin_p = _rope_tables(N_META + jnp.arange(seq, dtype=jnp.int32))
    gkv = norm_kv[None, :]
    g_mix1 = norm_mix[1][None, :]
    wkv_cat = _bf(jnp.concatenate([attn_wk, attn_wv], axis=1))
    wq = _bf(attn_wq[0])
    k_s, v_s, q_s = _kvq(hs, cos_s, sin_s, gkv, g_mix1, wkv_cat, wq, tm=m_s, rope_blocks=1)
    k_p, v_p, q_p = _kvq(hp, cos_p, sin_p, gkv, g_mix1, wkv_cat, wq, tm=tm, rope_blocks=seq // tm)

    sinks = attn_sinks[0]
    kv_all = NKV * HD

    def prompt_ctx(meta_rows, frames):
        meta = jnp.broadcast_to(meta_rows[None], (b_p, N_META, kv_all))
        rows = jnp.concatenate([meta, jnp.zeros((b_p, WINDOW, kv_all), F32),
                                frames.reshape(b_p, seq, kv_all)], axis=1)
        return _dup_kv_heads(_bf(rows))

    o_p = _attn_prompt(q_p, prompt_ctx(k_s[:N_META], k_p), prompt_ctx(v_s[:N_META], v_p), sinks,
                       n_streams=b_p, seq=seq, n_qc=4)

    n_cache = cache_win_k.shape[1]
    n_ctx = N_META + n_cache + t_s

    def short_ctx(cache_meta, cache_win, new):
        new = new.reshape(n_short, t_s, kv_all)
        sample = jnp.concatenate([cache_meta.reshape(b_s, N_META, kv_all),
                                  cache_win.reshape(b_s, n_cache, kv_all), new[1:]], axis=1)
        meta = jnp.concatenate([jnp.zeros((1, N_META + n_cache, kv_all), F32), new[:1]], axis=1)
        return _dup_kv_heads(_bf(jnp.concatenate([meta, sample], axis=0)))

    bias = jnp.zeros((n_short, 1, n_ctx), F32).at[0, 0, :N_META + n_cache].set(NEG_INF)
    o_s = _attn_short(q_s, short_ctx(cache_meta_k, cache_win_k, k_s), short_ctx(cache_meta_v, cache_win_v, v_s),
                      bias, sinks, n_streams=n_short, n_q=t_s)

    wo1 = _bf(attn_wo[0])
    g_ffn1 = norm_ffn[1][None, :]
    router = jnp.pad(moe_router[0], ((0, 0), (0, LANES - N_EXPERTS)))
    r_hi = _bf(router)
    r_lo = _bf(router - r_hi.astype(F32))
    hs, xn_s, comb_s = _attn_out_router(hs, o_s, wo1, g_ffn1, r_hi, r_lo, tm=m_s)
    hp, xn_p, comb_p = _attn_out_router(hp, o_p, wo1, g_ffn1, r_hi, r_lo, tm=tm)

    fc_e = 512
    wg_e = _bf(_chunk_cols(moe_w_gate[0], fc_e))
    wu_e = _bf(_chunk_cols(moe_w_up[0], fc_e))
    wd_e = _bf(moe_w_down[0].reshape(N_EXPERTS, -1, fc_e, D))
    gfin = norm_final[None, :]
    y_s = _moe(xn_s, comb_s, hs, gfin, wg_e, wu_e, wd_e, tm=m_s)
    y_p = _moe(xn_p, comb_p, hp, gfin, wg_e, wu_e, wd_e, tm=1024 if m_p % 1024 == 0 else tm)

    y_prompt = y_p.reshape(b_p, seq, D)
    y_sample = y_s[N_META:].reshape(b_s, t_s, D)
    new_wkv_p = s_prompt[:, None]
    new_wkv_s = s_short[1:][:, None]
    k_meta = k_s[:N_META].reshape(N_META, NKV, HD)
    v_meta = v_s[:N_META].reshape(N_META, NKV, HD)
    new_meta_k_p = jnp.broadcast_to(k_meta[None], (b_p, N_META, NKV, HD))
    new_meta_v_p = jnp.broadcast_to(v_meta[None], (b_p, N_META, NKV, HD))
    new_win_k_p = k_p.reshape(b_p, seq, NKV, HD)[:, seq - WINDOW:]
    new_win_v_p = v_p.reshape(b_p, seq, NKV, HD)[:, seq - WINDOW:]
    new_k_s = k_s[N_META:].reshape(b_s, t_s, NKV, HD)
    new_v_s = v_s[N_META:].reshape(b_s, t_s, NKV, HD)
    return (y_prompt, y_sample, new_wkv_p, new_shift_p, new_meta_k_p, new_meta_v_p, new_win_k_p, new_win_v_p,
            new_wkv_s, new_shift_s, new_k_s, new_v_s)
```
